```python
import math
import jax, jax.numpy as jnp
from jax import lax
import numpy as np

D_MODEL = 2048
BATCH = 8
SEQ = 2048
DEPTH = 2

GRID_W = 64
CTX_LEN = 256
MIX_W = D_MODEL
DIFF_W = MIX_W // 4
GQA_W = MIX_W // 2
RET_W = MIX_W - DIFF_W - GQA_W
DIFF_DV = 128
DIFF_DK = DIFF_DV // 2
DIFF_HEADS = DIFF_W // DIFF_DV
GQA_DH = 128
GQA_HEADS = GQA_W // GQA_DH
GQA_KV_HEADS = GQA_HEADS // 4
GQA_GROUP = GQA_HEADS // GQA_KV_HEADS
RET_DV = 128
RET_DK = RET_DV // 2
RET_HEADS = RET_W // RET_DV
RET_CHUNK = 128
Q_BLOCK = 128
ROPE_BASE = 10000.0
NORM_EPS = 1e-6
N_KEYS = 128
N_EXPERTS = N_KEYS * N_KEYS
PEER_HEADS = 8
PEER_TOPK = 16
PEER_QDIM = 256
PEER_HALF = PEER_QDIM // 2
PEER_TOKEN_BLOCK = 128
Q_SIZES = (DIFF_HEADS * 2 * DIFF_DK, GQA_HEADS * GQA_DH, RET_HEADS * RET_DK, RET_W)
KV_SIZES = (DIFF_HEADS * 2 * DIFF_DK, DIFF_W, GQA_KV_HEADS * GQA_DH, GQA_KV_HEADS * GQA_DH, RET_HEADS * RET_DK, RET_W)
Q_COLS = sum(Q_SIZES)
KV_COLS = sum(KV_SIZES)
IN_COLS = Q_COLS + KV_COLS

kernel_name = "hybrid_diff_gqa_retention_peer_dit"


def rms_norm(x, g):
    xf = x.astype(jnp.float32)
    y = xf * lax.rsqrt(jnp.mean(xf * xf, axis=-1, keepdims=True) + NORM_EPS)
    return (y * g.astype(jnp.float32)).astype(x.dtype)


def modulate(x, g, shift, scale):
    return rms_norm(x, g) * (1 + scale) + shift


def split_cols(a, sizes):
    offsets, acc = [], 0
    for s in sizes[:-1]:
        acc += s
        offsets.append(acc)
    return jnp.split(a, offsets, axis=-1)


def split_heads(a, n_heads, dh):
    b, t = a.shape[:2]
    return a.reshape(b, t, n_heads, dh).transpose(0, 2, 1, 3)


def merge_heads(a):
    b, h, t, d = a.shape
    return a.transpose(0, 2, 1, 3).reshape(b, t, h * d)


def axial_rope(rows, head_dim):
    t = jnp.arange(rows * GRID_W)
    row = (t // GRID_W).astype(jnp.float32)
    col = (t % GRID_W).astype(jnp.float32)
    d_axis = head_dim // 2
    inv_freq = ROPE_BASE ** (-jnp.arange(0, d_axis, 2, dtype=jnp.float32) / d_axis)
    ang = jnp.concatenate([row[:, None] * inv_freq, col[:, None] * inv_freq], axis=-1)
    return jnp.cos(ang), jnp.sin(ang)


def apply_rope(x, cos, sin):
    xf = x.astype(jnp.float32)
    x1, x2 = xf[..., 0::2], xf[..., 1::2]
    out = jnp.stack([x1 * cos - x2 * sin, x1 * sin + x2 * cos], axis=-1).reshape(x.shape)
    return out.astype(x.dtype)


def sweep_query_blocks(fn, q):
    n_q = q.shape[-2]
    qb = q.reshape(q.shape[:-2] + (n_q // Q_BLOCK, Q_BLOCK, q.shape[-1]))
    qb = jnp.moveaxis(qb, -3, 0)
    ob = lax.map(fn, qb)
    ob = jnp.moveaxis(ob, 0, -3)
    return ob.reshape(ob.shape[:-3] + (n_q, ob.shape[-1]))


def diff_attention(q, k, v, lam):
    scale = DIFF_DK ** -0.5

    def block(qb):
        s = jnp.einsum('bhmqd,bhmkd->bhmqk', qb, k).astype(jnp.float32) * scale
        a = jax.nn.softmax(s, axis=-1)
        a = a[:, :, 0] - lam * a[:, :, 1]
        return jnp.einsum('bhqk,bhkd->bhqd', a.astype(v.dtype), v)

    return sweep_query_blocks(block, q)


def gqa_attention(q, k, v):
    scale = GQA_DH ** -0.5

    def block(qb):
        s = jnp.einsum('bhgqd,bhkd->bhgqk', qb, k).astype(jnp.float32) * scale
        a = jax.nn.softmax(s, axis=-1)
        return jnp.einsum('bhgqk,bhkd->bhgqd', a.astype(v.dtype), v)

    return sweep_query_blocks(block, q)


def retention_scan(q, k, v, log_gamma, state0, include_diag):
    b, h, t, _ = q.shape
    dv = v.shape[-1]
    n_chunks = t // RET_CHUNK
    pos = jnp.arange(RET_CHUNK, dtype=jnp.float32)
    rel = pos[:, None] - pos[None, :]
    keep = (rel >= 0) if include_diag else (rel > 0)
    inner_decay = jnp.where(keep, jnp.exp(log_gamma[:, None, None] * jnp.where(keep, rel, 0.0)), 0.0)
    q_decay = jnp.exp(log_gamma[:, None] * (pos + 1.0))[..., None]
    k_decay = jnp.exp(log_gamma[:, None] * (RET_CHUNK - 1.0 - pos))[..., None]
    chunk_decay = jnp.exp(log_gamma * RET_CHUNK)[:, None, None]

    def to_chunks(a):
        return jnp.moveaxis(a.reshape(b, h, n_chunks, RET_CHUNK, a.shape[-1]), 2, 0)

    def step(state, qkv):
        qc, kc, vc = qkv
        attn = jnp.einsum('bhid,bhjd->bhij', qc, kc) * inner_decay
        out = (jnp.einsum('bhij,bhjv->bhiv', attn, vc)
               + jnp.einsum('bhid,bhdv->bhiv', qc * q_decay, state))
        state = state * chunk_decay + jnp.einsum('bhjd,bhjv->bhdv', kc * k_decay, vc)
        return state, out

    _, out = lax.scan(step, state0, (to_chunks(q), to_chunks(k), to_chunks(v)))
    return jnp.moveaxis(out, 0, 2).reshape(b, h, t, dv)


def retention_final_state(k, v, log_gamma, reverse):
    kf, vf = k.astype(jnp.float32), v.astype(jnp.float32)
    t = jnp.arange(k.shape[2], dtype=jnp.float32)
    dist = t if reverse else (k.shape[2] - 1.0 - t)
    w = jnp.exp(log_gamma[:, None] * dist[None, :])[..., None]
    return jnp.einsum('bhtd,bhtv->bhdv', kf * w, vf)


def retention_bidir(q, k, v, gate, log_g, state_f, state_b, out_g):
    qf, kf, vf = q.astype(jnp.float32), k.astype(jnp.float32), v.astype(jnp.float32)
    fwd = retention_scan(qf, kf, vf, log_g[0], state_f, True)
    bwd = jnp.flip(retention_scan(jnp.flip(qf, 2), jnp.flip(kf, 2), jnp.flip(vf, 2), log_g[1], state_b, False), 2)
    y = merge_heads(rms_norm(fwd + bwd, out_g)).astype(v.dtype)
    return jax.nn.silu(gate) * y


def diff_qk_heads(a, g):
    b, t = a.shape[:2]
    a = a.reshape(b, t, DIFF_HEADS, 2, DIFF_DK).transpose(0, 2, 3, 1, 4)
    return rms_norm(a, g)


def q_heads(q_side, rope_diff, rope_gqa, diff_q_norm_g, gqa_q_norm_g):
    dq, gq, rq, rgate = split_cols(q_side, Q_SIZES)
    b, t = q_side.shape[:2]
    dq = diff_qk_heads(dq, diff_q_norm_g)
    gq = rms_norm(gq.reshape(b, t, GQA_KV_HEADS, GQA_GROUP, GQA_DH).transpose(0, 2, 3, 1, 4), gqa_q_norm_g)
    if rope_diff is not None:
        dq = apply_rope(dq, *rope_diff)
        gq = apply_rope(gq, *rope_gqa)
    return dq, gq, split_heads(rq, RET_HEADS, RET_DK), rgate


def kv_heads(kv_side, rope_diff, rope_gqa, diff_k_norm_g, gqa_k_norm_g):
    dk, dv, gk, gv, rk, rv = split_cols(kv_side, KV_SIZES)
    dk = diff_qk_heads(dk, diff_k_norm_g)
    gk = rms_norm(split_heads(gk, GQA_KV_HEADS, GQA_DH), gqa_k_norm_g)
    if rope_diff is not None:
        dk = apply_rope(dk, *rope_diff)
        gk = apply_rope(gk, *rope_gqa)
    return (dk, split_heads(dv, DIFF_HEADS, DIFF_DV), gk, split_heads(gv, GQA_KV_HEADS, GQA_DH),
            split_heads(rk, RET_HEADS, RET_DK) * (RET_DK ** -0.5), split_heads(rv, RET_HEADS, RET_DV))


def mix_tokens(qh, attn_kv, ret_kv, ret_states, lam, lam_init, log_g, w_out, diff_out_norm_g, ret_out_norm_g):
    dq, gq, rq, rgate = qh
    dk, dv, gk, gv = attn_kv
    rk, rv = ret_kv
    o_diff = merge_heads(rms_norm(diff_attention(dq, dk, dv, lam), diff_out_norm_g) * (1.0 - lam_init))
    o_gqa = gqa_attention(gq, gk, gv)
    b, _, _, t, _ = o_gqa.shape
    o_gqa = o_gqa.transpose(0, 3, 1, 2, 4).reshape(b, t, GQA_W)
    o_ret = retention_bidir(rq, rk, rv, rgate, log_g, ret_states[0], ret_states[1], ret_out_norm_g)
    return jnp.concatenate([o_diff, o_gqa, o_ret], axis=-1) @ w_out


def peer(h, w_q, sub_keys, u, v):
    b, t, d = h.shape
    hf = h.reshape(-1, d)
    n_tok = hf.shape[0]
    q = (hf @ w_q).reshape(n_tok, PEER_HEADS, 2, PEER_HALF)
    s = jnp.einsum('nhpd,hpkd->nhpk', q, sub_keys).astype(jnp.float32)
    s1, i1 = lax.top_k(s[:, :, 0], PEER_TOPK)
    s2, i2 = lax.top_k(s[:, :, 1], PEER_TOPK)
    cand = (s1[..., :, None] + s2[..., None, :]).reshape(n_tok, PEER_HEADS, PEER_TOPK * PEER_TOPK)
    cand_idx = (i1[..., :, None] * N_KEYS + i2[..., None, :]).reshape(n_tok, PEER_HEADS, PEER_TOPK * PEER_TOPK)
    top_s, pos = lax.top_k(cand, PEER_TOPK)
    idx = jnp.take_along_axis(cand_idx, pos, axis=-1)
    gates = jax.nn.softmax(top_s, axis=-1).astype(h.dtype)
    nb = n_tok // PEER_TOKEN_BLOCK

    def block(args):
        xb, ib, gb = args
        act = jax.nn.gelu(jnp.einsum('td,thkd->thk', xb, u[ib]), approximate=False) * gb
        return jnp.einsum('thk,thkd->td', act, v[ib])

    out = lax.map(block, (hf.reshape(nb, PEER_TOKEN_BLOCK, d),
                          idx.reshape(nb, PEER_TOKEN_BLOCK, PEER_HEADS, PEER_TOPK),
                          gates.reshape(nb, PEER_TOKEN_BLOCK, PEER_HEADS, PEER_TOPK)))
    return out.reshape(b, t, d)


def trunk_layer(x, ctx, c, c_ctx, layer_idx, update_ctx, rope_diff, rope_gqa,
                norm1_g, norm2_g, w_mod, b_mod, w_in, w_out,
                diff_q_norm_g, diff_k_norm_g, diff_lambda, diff_out_norm_g,
                gqa_q_norm_g, gqa_k_norm_g, ret_decay_logit, ret_out_norm_g,
                peer_w_q, peer_sub_keys, peer_u, peer_v):
    lam_init = 0.8 - 0.6 * math.exp(-0.3 * layer_idx)
    lp = diff_lambda.astype(jnp.float32)
    lam = jnp.exp(jnp.sum(lp[0] * lp[1])) - jnp.exp(jnp.sum(lp[2] * lp[3])) + lam_init
    log_g = jax.nn.log_sigmoid(ret_decay_logit.astype(jnp.float32))

    mod = jax.nn.silu(c) @ w_mod + b_mod
    sh1, sc1, g1, sh2, sc2, g2 = jnp.split(mod[:, None, :], 6, axis=-1)
    n_ctx_mod = 6 if update_ctx else 2
    mods_c = jnp.split(jax.nn.silu(c_ctx) @ w_mod[:, :n_ctx_mod * D_MODEL] + b_mod[:n_ctx_mod * D_MODEL], n_ctx_mod)

    h = modulate(x, norm1_g, sh1, sc1)
    hc = modulate(ctx, norm1_g, mods_c[0], mods_c[1])

    kv_c = kv_heads(hc @ w_in[:, Q_COLS:], None, None, diff_k_norm_g, gqa_k_norm_g)
    ctx_states = (retention_final_state(kv_c[4], kv_c[5], log_g[0], False),
                  retention_final_state(kv_c[4], kv_c[5], log_g[1], True))

    q_l = q_heads(h @ w_in[:, :Q_COLS], rope_diff, rope_gqa, diff_q_norm_g, gqa_q_norm_g)
    kv_l = kv_heads(h @ w_in[:, Q_COLS:], rope_diff, rope_gqa, diff_k_norm_g, gqa_k_norm_g)
    attn_kv_l = tuple(jnp.concatenate([a_c, a_l], axis=-2) for a_c, a_l in zip(kv_c[:4], kv_l[:4]))
    mix_l = mix_tokens(q_l, attn_kv_l, kv_l[4:], ctx_states, lam, lam_init, log_g, w_out,
                       diff_out_norm_g, ret_out_norm_g)
    x = x + g1 * mix_l
    x = x + g2 * peer(modulate(x, norm2_g, sh2, sc2), peer_w_q, peer_sub_keys, peer_u, peer_v)

    if update_ctx:
        q_c = q_heads(hc @ w_in[:, :Q_COLS], None, None, diff_q_norm_g, gqa_q_norm_g)
        zero = jnp.zeros((ctx.shape[0], RET_HEADS, RET_DK, RET_DV), jnp.float32)
        mix_c = mix_tokens(q_c, kv_c[:4], kv_c[4:], (zero, zero), lam, lam_init, log_g, w_out,
                           diff_out_norm_g, ret_out_norm_g)
        ctx = ctx + mods_c[2] * mix_c
        ctx = ctx + mods_c[5] * peer(modulate(ctx, norm2_g, mods_c[3], mods_c[4]),
                                     peer_w_q, peer_sub_keys, peer_u, peer_v)
    return x, ctx


def setup_inputs(seed: int = 0) -> dict:
    key = jax.random.key(seed)
    ks = jax.random.split(key, 24)
    f32 = jnp.float32

    def nrm(k, shape, s):
        return jax.random.normal(k, shape, f32) * s

    def gain(k, shape):
        return 1.0 + 0.02 * jax.random.normal(k, shape, f32)

    base_logit = jnp.log(2.0 ** (5.0 + jnp.arange(RET_HEADS, dtype=f32)) - 1.0)
    return {
        "x": nrm(ks[0], (BATCH, SEQ, D_MODEL), 1.0),
        "c": nrm(ks[1], (BATCH, D_MODEL), 1.0),
        "ctx": nrm(ks[2], (BATCH, CTX_LEN, D_MODEL), 1.0),
        "c_ctx": nrm(ks[3], (D_MODEL,), 1.0),
        "norm1_g": gain(ks[4], (DEPTH, D_MODEL)),
        "norm2_g": gain(ks[5], (DEPTH, D_MODEL)),
        "w_mod": nrm(ks[6], (DEPTH, D_MODEL, 6 * D_MODEL), 0.5 * D_MODEL ** -0.5),
        "b_mod": nrm(ks[7], (DEPTH, 6 * D_MODEL), 0.01),
        "w_in": nrm(ks[8], (DEPTH, D_MODEL, IN_COLS), D_MODEL ** -0.5),
        "w_out": nrm(ks[9], (DEPTH, MIX_W, D_MODEL), MIX_W ** -0.5),
        "diff_q_norm_g": gain(ks[10], (DEPTH, DIFF_DK)),
        "diff_k_norm_g": gain(ks[11], (DEPTH, DIFF_DK)),
        "diff_lambda": nrm(ks[12], (DEPTH, 4, DIFF_DK), 0.1),
        "diff_out_norm_g": gain(ks[13], (DEPTH, DIFF_DV)),
        "gqa_q_norm_g": gain(ks[14], (DEPTH, GQA_DH)),
        "gqa_k_norm_g": gain(ks[15], (DEPTH, GQA_DH)),
        "ret_decay_logit": base_logit + nrm(ks[16], (DEPTH, 2, RET_HEADS), 0.05),
        "ret_out_norm_g": gain(ks[17], (DEPTH, RET_DV)),
        "peer_w_q": nrm(ks[18], (DEPTH, D_MODEL, PEER_HEADS * PEER_QDIM), D_MODEL ** -0.5),
        "peer_sub_keys": nrm(ks[19], (DEPTH, PEER_HEADS, 2, N_KEYS, PEER_HALF), PEER_HALF ** -0.5),
        "peer_u": nrm(ks[20], (DEPTH, N_EXPERTS, D_MODEL), D_MODEL ** -0.5),
        "peer_v": nrm(ks[21], (DEPTH, N_EXPERTS, D_MODEL), 1.0),
    }


def reference(x, c, ctx, c_ctx, norm1_g, norm2_g, w_mod, b_mod, w_in, w_out,
              diff_q_norm_g, diff_k_norm_g, diff_lambda, diff_out_norm_g,
              gqa_q_norm_g, gqa_k_norm_g, ret_decay_logit, ret_out_norm_g,
              peer_w_q, peer_sub_keys, peer_u, peer_v):
    rows = x.shape[1] // GRID_W
    rope_diff = axial_rope(rows, DIFF_DK)
    rope_gqa = axial_rope(rows, GQA_DH)
    for l in range(DEPTH):
        x, ctx = trunk_layer(x, ctx, c, c_ctx, l, l < DEPTH - 1, rope_diff, rope_gqa,
                             norm1_g[l], norm2_g[l], w_mod[l], b_mod[l], w_in[l], w_out[l],
                             diff_q_norm_g[l], diff_k_norm_g[l], diff_lambda[l], diff_out_norm_g[l],
                             gqa_q_norm_g[l], gqa_k_norm_g[l], ret_decay_logit[l], ret_out_norm_g[l],
                             peer_w_q[l], peer_sub_keys[l], peer_u[l], peer_v[l])
    return x
```

```python
import functools
import math

import jax
import jax.numpy as jnp
from jax import lax
from jax.experimental import pallas as pl
from jax.experimental.pallas import tpu as pltpu

F32 = jnp.float32
MXU_DTYPE = jnp.bfloat16

LANES = 128
GRID_W = 64
ROPE_BASE = 10000.0
NORM_EPS = 1e-6
HEAD_W = 128
HALF_W = HEAD_W // 2
GQA_GROUP = 4
RET_CHUNK = 128
PEER_TOPK = 16
N_KEYS = 128
VMEM_LIMIT = 56 * 1024 * 1024


def _cparams(*sem):
    return pltpu.CompilerParams(dimension_semantics=sem, vmem_limit_bytes=VMEM_LIMIT)


def _mx(a):
    return a.astype(MXU_DTYPE)


def _dot(a, b):
    return jnp.dot(_mx(a), _mx(b), preferred_element_type=F32)


def _dot_nt(a, b):
    return lax.dot_general(_mx(a), _mx(b), (((1,), (1,)), ((), ())), preferred_element_type=F32)


def _lane_iota(shape):
    return lax.broadcasted_iota(jnp.int32, shape, len(shape) - 1)


def _rms_full(x, g):
    ms = jnp.mean(x * x, axis=-1, keepdims=True)
    return x * lax.rsqrt(ms + NORM_EPS) * g


def _rms_halves(x, g):
    x2 = x * x
    lo_mask = _lane_iota(x.shape) < HALF_W
    tot = jnp.sum(x2, axis=-1, keepdims=True)
    lo = jnp.sum(jnp.where(lo_mask, x2, 0.0), axis=-1, keepdims=True)
    ms = jnp.where(lo_mask, lo, tot - lo) * (1.0 / HALF_W)
    return x * lax.rsqrt(ms + NORM_EPS) * g


def _rope(x, cos_full, sin_signed):
    n = x.shape[-1]
    ax = x.ndim - 1
    nxt = pltpu.roll(x, n - 1, axis=ax)
    prv = pltpu.roll(x, 1, axis=ax)
    swapped = jnp.where((_lane_iota(x.shape) & 1) == 0, nxt, prv)
    return x * cos_full + swapped * sin_signed


def _mod_body(c_ref, w_ref, b_ref, o_ref):
    c = c_ref[...]
    a = c * jax.nn.sigmoid(c)
    o_ref[...] = jnp.dot(a, w_ref[...], preferred_element_type=F32) + b_ref[...]


def _modulation(c16, w_mod, b_mod):
    d, n = w_mod.shape
    tn = 512
    return pl.pallas_call(
        _mod_body,
        grid=(n // tn,),
        in_specs=[pl.BlockSpec((16, d), lambda j: (0, 0)),
                  pl.BlockSpec((d, tn), lambda j: (0, j)),
                  pl.BlockSpec((1, tn), lambda j: (0, j))],
        out_specs=pl.BlockSpec((16, tn), lambda j: (0, j)),
        out_shape=jax.ShapeDtypeStruct((16, n), F32),
        compiler_params=_cparams("arbitrary"),
        name="modulation",
    )(c16, w_mod, b_mod.reshape(1, n))


def _nm_body(x_ref, g_ref, sh_ref, sc_ref, w_ref, o_ref, *rest, emit_h):
    if emit_h:
        hout_ref, h_s = rest
    else:
        (h_s,) = rest

    @pl.when(pl.program_id(2) == 0)
    def _():
        h = _rms_full(x_ref[0], g_ref[...]) * (1.0 + sc_ref[0]) + sh_ref[0]
        h_s[...] = h.astype(h_s.dtype)
        if emit_h:
            hout_ref[0] = h.astype(hout_ref.dtype)

    o_ref[0] = jnp.dot(h_s[...], w_ref[...], preferred_element_type=F32)


def _normmod_matmul(x, g, sh, sc, w, *, per_batch_mod, emit_h, tn):
    b, t, d = x.shape
    n = w.shape[1]
    tm = min(t, 1024)
    mod_map = (lambda bi, i, j: (bi, 0, 0)) if per_batch_mod else (lambda bi, i, j: (0, 0, 0))
    out_shape = [jax.ShapeDtypeStruct((b, t, n), F32)]
    out_specs = [pl.BlockSpec((1, tm, tn), lambda bi, i, j: (bi, i, j))]
    if emit_h:
        out_shape.append(jax.ShapeDtypeStruct((b, t, d), MXU_DTYPE))
        out_specs.append(pl.BlockSpec((1, tm, d), lambda bi, i, j: (bi, i, 0)))
    return pl.pallas_call(
        functools.partial(_nm_body, emit_h=emit_h),
        grid=(b, t // tm, n // tn),
        in_specs=[pl.BlockSpec((1, tm, d), lambda bi, i, j: (bi, i, 0)),
                  pl.BlockSpec((1, d), lambda bi, i, j: (0, 0)),
                  pl.BlockSpec((1, 1, d), mod_map),
                  pl.BlockSpec((1, 1, d), mod_map),
                  pl.BlockSpec((d, tn), lambda bi, i, j: (0, j))],
        out_specs=out_specs,
        out_shape=out_shape,
        scratch_shapes=[pltpu.VMEM((tm, d), MXU_DTYPE)],
        compiler_params=_cparams("parallel", "parallel", "arbitrary"),
        name="normmod_proj",
    )(x, g.reshape(1, d), sh, sc, w)


def _softmax_parts(s):
    m = jnp.max(s, axis=-1, keepdims=True)
    p = jnp.exp(s - m)
    return p, 1.0 / jnp.sum(p, axis=-1, keepdims=True)


def _diff_body(*refs, has_lat, lam_init, t_ctx):
    if has_lat:
        (lp_ref, q_ref, kc_ref, vc_ref, kl_ref, vl_ref, cq_ref, sq_ref, ck_ref, sk_ref,
         gq_ref, gk_ref, go_ref, o_ref, k_s, v_s) = refs
    else:
        lp_ref, q_ref, kc_ref, vc_ref, gq_ref, gk_ref, go_ref, o_ref, k_s, v_s = refs

    @pl.when(pl.program_id(2) == 0)
    def _():
        k_s[0:t_ctx, :] = _rms_halves(kc_ref[0], gk_ref[...]).astype(k_s.dtype)
        v_s[0:t_ctx, :] = vc_ref[0].astype(v_s.dtype)
        if has_lat:
            kl = _rope(_rms_halves(kl_ref[0], gk_ref[...]), ck_ref[...], sk_ref[...])
            k_s[t_ctx:, :] = kl.astype(k_s.dtype)
            v_s[t_ctx:, :] = vl_ref[0].astype(v_s.dtype)

    lp = lp_ref[...]
    lam = (jnp.exp(jnp.sum(lp[0:1] * lp[1:2], axis=-1, keepdims=True))
           - jnp.exp(jnp.sum(lp[2:3] * lp[3:4], axis=-1, keepdims=True)) + lam_init)

    q = _rms_halves(q_ref[0], gq_ref[...])
    if has_lat:
        q = _rope(q, cq_ref[...], sq_ref[...])
    q = q * (HALF_W ** -0.5)
    lo_mask = _lane_iota(q.shape) < HALF_W
    k = k_s[...]
    p0, r0 = _softmax_parts(_dot_nt(jnp.where(lo_mask, q, 0.0), k))
    p1, r1 = _softmax_parts(_dot_nt(jnp.where(lo_mask, 0.0, q), k))
    a = p0 * r0 - lam * (p1 * r1)
    o = _dot(a, v_s[...])
    o_ref[0] = (_rms_full(o, go_ref[...]) * (1.0 - lam_init)).astype(o_ref.dtype)


def _diff_attention(lp, q_arr, q_blk0, lat, ctx, kc_blk0, vc_blk0, rope, gq, gk, go, *, lam_init, n_heads):
    b, tq_all, _ = q_arr.shape
    t_ctx = ctx.shape[1]
    has_lat = lat is not None
    t_lat = lat.shape[1] if has_lat else 0
    tq = min(tq_all, 256)
    col = lambda blk0: (lambda bi, h, i: (bi, 0, blk0 + h))
    vec = pl.BlockSpec((1, LANES), lambda bi, h, i: (0, 0))
    in_specs = [pl.BlockSpec(lp.shape, lambda bi, h, i: (0, 0)),
                pl.BlockSpec((1, tq, LANES), lambda bi, h, i: (bi, i, q_blk0 + h)),
                pl.BlockSpec((1, t_ctx, LANES), col(kc_blk0)),
                pl.BlockSpec((1, t_ctx, LANES), col(vc_blk0))]
    args = [lp, q_arr, ctx, ctx]
    if has_lat:
        cos, sin = rope
        in_specs += [pl.BlockSpec((1, t_lat, LANES), col(kc_blk0)),
                     pl.BlockSpec((1, t_lat, LANES), col(vc_blk0)),
                     pl.BlockSpec((tq, LANES), lambda bi, h, i: (i, 0)),
                     pl.BlockSpec((tq, LANES), lambda bi, h, i: (i, 0)),
                     pl.BlockSpec((t_lat, LANES), lambda bi, h, i: (0, 0)),
                     pl.BlockSpec((t_lat, LANES), lambda bi, h, i: (0, 0))]
        args += [lat, lat, cos, sin, cos, sin]
    in_specs += [vec, vec, vec]
    args += [gq, gk, go]
    return pl.pallas_call(
        functools.partial(_diff_body, has_lat=has_lat, lam_init=lam_init, t_ctx=t_ctx),
        grid=(b, n_heads, tq_all // tq),
        in_specs=in_specs,
        out_specs=pl.BlockSpec((1, tq, LANES), lambda bi, h, i: (bi, i, h)),
        out_shape=jax.ShapeDtypeStruct((b, tq_all, n_heads * LANES), MXU_DTYPE),
        scratch_shapes=[pltpu.VMEM((t_ctx + t_lat, LANES), MXU_DTYPE),
                        pltpu.VMEM((t_ctx + t_lat, LANES), MXU_DTYPE)],
        compiler_params=_cparams("parallel", "parallel", "arbitrary"),
        name="diff_attention",
    )(*args)


def _gqa_body(*refs, has_lat, t_ctx):
    q_refs = refs[:GQA_GROUP]
    refs = refs[GQA_GROUP:]
    if has_lat:
        (kc_ref, vc_ref, kl_ref, vl_ref, cq_ref, sq_ref, ck_ref, sk_ref,
         gq_ref, gk_ref, o_ref, k_s, v_s) = refs
    else:
        kc_ref, vc_ref, gq_ref, gk_ref, o_ref, k_s, v_s = refs

    @pl.when(pl.program_id(2) == 0)
    def _():
        k_s[0:t_ctx, :] = _rms_full(kc_ref[0], gk_ref[...]).astype(k_s.dtype)
        v_s[0:t_ctx, :] = vc_ref[0].astype(v_s.dtype)
        if has_lat:
            kl = _rope(_rms_full(kl_ref[0], gk_ref[...]), ck_ref[...], sk_ref[...])
            k_s[t_ctx:, :] = kl.astype(k_s.dtype)
            v_s[t_ctx:, :] = vl_ref[0].astype(v_s.dtype)

    k = k_s[...]
    v = v_s[...]
    for g in range(GQA_GROUP):
        q = _rms_full(q_refs[g][0], gq_ref[...])
        if has_lat:
            q = _rope(q, cq_ref[...], sq_ref[...])
        p, r = _softmax_parts(_dot_nt(q * (HEAD_W ** -0.5), k))
        o_ref[0, :, g * LANES:(g + 1) * LANES] = (_dot(p, v) * r).astype(o_ref.dtype)


def _gqa_attention(q_arr, q_blk0, lat, ctx, k_blk0, v_blk0, rope, gq, gk, *, n_kv):
    b, tq_all, _ = q_arr.shape
    t_ctx = ctx.shape[1]
    has_lat = lat is not None
    t_lat = lat.shape[1] if has_lat else 0
    tq = min(tq_all, 256)
    col = lambda blk0: (lambda bi, h, i: (bi, 0, blk0 + h))
    vec = pl.BlockSpec((1, LANES), lambda bi, h, i: (0, 0))

    def q_spec(g):
        return pl.BlockSpec((1, tq, LANES), lambda bi, h, i: (bi, i, q_blk0 + h * GQA_GROUP + g))

    in_specs = [q_spec(g) for g in range(GQA_GROUP)]
    args = [q_arr] * GQA_GROUP
    in_specs += [pl.BlockSpec((1, t_ctx, LANES), col(k_blk0)),
                 pl.BlockSpec((1, t_ctx, LANES), col(v_blk0))]
    args += [ctx, ctx]
    if has_lat:
        cos, sin = rope
        in_specs += [pl.BlockSpec((1, t_lat, LANES), col(k_blk0)),
                     pl.BlockSpec((1, t_lat, LANES), col(v_blk0)),
                     pl.BlockSpec((tq, LANES), lambda bi, h, i: (i, 0)),
                     pl.BlockSpec((tq, LANES), lambda bi, h, i: (i, 0)),
                     pl.BlockSpec((t_lat, LANES), lambda bi, h, i: (0, 0)),
                     pl.BlockSpec((t_lat, LANES), lambda bi, h, i: (0, 0))]
        args += [lat, lat, cos, sin, cos, sin]
    in_specs += [vec, vec]
    args += [gq, gk]
    gw = GQA_GROUP * LANES
    return pl.pallas_call(
        functools.partial(_gqa_body, has_lat=has_lat, t_ctx=t_ctx),
        grid=(b, n_kv, tq_all // tq),
        in_specs=in_specs,
        out_specs=pl.BlockSpec((1, tq, gw), lambda bi, h, i: (bi, i, h)),
        out_shape=jax.ShapeDtypeStruct((b, tq_all, n_kv * gw), MXU_DTYPE),
        scratch_shapes=[pltpu.VMEM((t_ctx + t_lat, LANES), MXU_DTYPE),
                        pltpu.VMEM((t_ctx + t_lat, LANES), MXU_DTYPE)],
        compiler_params=_cparams("parallel", "parallel", "arbitrary"),
        name="gqa_attention",
    )(*args)


def _ret_body(*refs, has_ctx, n_chunks):
    if has_ctx:
        dl_ref, q_ref, k_ref, v_ref, gate_ref, kc_ref, vc_ref, go_ref, o_ref, acc_s = refs
    else:
        dl_ref, q_ref, k_ref, v_ref, gate_ref, go_ref, o_ref, acc_s = refs
    c_len = RET_CHUNK
    half = pl.program_id(1) % 2
    sq = (c_len, LANES)
    lane = _lane_iota(sq)
    head_mask = (lane >= half * HALF_W) & (lane < (half + 1) * HALF_W)
    row = lax.broadcasted_iota(jnp.int32, sq, 0).astype(F32)
    rel = row - lane.astype(F32)
    lg = jax.nn.log_sigmoid(dl_ref[...])
    lgf = lg[0, 0, 0:1, :]
    lgb = lg[1, 0, 0:1, :]
    dec_f = jnp.where(rel >= 0, jnp.exp(lgf * jnp.maximum(rel, 0.0)), 0.0)
    dec_b = jnp.where(rel < 0, jnp.exp(lgb * jnp.maximum(-rel, 0.0)), 0.0)
    qd_f = jnp.exp(lgf * (row + 1.0))
    kd_f = jnp.exp(lgf * (c_len - 1.0 - row))
    qd_b = jnp.exp(lgb * (c_len - row))
    kd_b = jnp.exp(lgb * row)
    cd_f = jnp.exp(lgf * c_len)
    cd_b = jnp.exp(lgb * c_len)

    def qk(ref, r, scale):
        return jnp.where(head_mask, ref[0, r, :], 0.0) * scale

    k_scale = HALF_W ** -0.5
    if has_ctx:
        t_ctx = kc_ref.shape[1]
        trow = lax.broadcasted_iota(jnp.int32, (t_ctx, LANES), 0).astype(F32)
        kc = jnp.where(head_mask[0:1], kc_ref[0], 0.0) * k_scale
        vc = vc_ref[0]
        s_f = _dot((kc * jnp.exp(lgf * (t_ctx - 1.0 - trow))).T, vc)
        s_b = _dot((kc * jnp.exp(lgb * trow)).T, vc)
    else:
        s_f = jnp.zeros(sq, F32)
        s_b = jnp.zeros(sq, F32)

    def step(c, s, dec, qd, kd, cd, first):
        r = pl.ds(pl.multiple_of(c * c_len, c_len), c_len)
        q = qk(q_ref, r, 1.0)
        k = qk(k_ref, r, k_scale)
        v = v_ref[0, r, :]
        out = _dot(_dot_nt(q, k) * dec, v) + _dot(q * qd, s)
        if first:
            acc_s[r, :] = out
        else:
            acc_s[r, :] += out
        return s * cd + _dot((k * kd).T, v)

    lax.fori_loop(0, n_chunks, lambda c, s: step(c, s, dec_f, qd_f, kd_f, cd_f, True), s_f)
    lax.fori_loop(0, n_chunks,
                  lambda c, s: step(n_chunks - 1 - c, s, dec_b, qd_b, kd_b, cd_b, False), s_b)

    gate = gate_ref[0]
    y = _rms_full(acc_s[...], go_ref[...])
    o_ref[0] = (gate * jax.nn.sigmoid(gate) * y).astype(o_ref.dtype)


def _retention(dl, src, ctx, q_blk0, gate_blk0, k_blk0, v_blk0, go, *, n_heads):
    b, t, _ = src.shape
    has_ctx = ctx is not None
    col = lambda blk0, per_head: (
        (lambda bi, h: (bi, 0, blk0 + h)) if per_head else (lambda bi, h: (bi, 0, blk0 + h // 2)))
    in_specs = [pl.BlockSpec((2, 1, 8, LANES), lambda bi, h: (0, h, 0, 0)),
                pl.BlockSpec((1, t, LANES), col(q_blk0, False)),
                pl.BlockSpec((1, t, LANES), col(k_blk0, False)),
                pl.BlockSpec((1, t, LANES), col(v_blk0, True)),
                pl.BlockSpec((1, t, LANES), col(gate_blk0, True))]
    args = [dl, src, src, src, src]
    if has_ctx:
        t_ctx = ctx.shape[1]
        in_specs += [pl.BlockSpec((1, t_ctx, LANES), col(k_blk0, False)),
                     pl.BlockSpec((1, t_ctx, LANES), col(v_blk0, True))]
        args += [ctx, ctx]
    in_specs.append(pl.BlockSpec((1, LANES), lambda bi, h: (0, 0)))
    args.append(go)
    return pl.pallas_call(
        functools.partial(_ret_body, has_ctx=has_ctx, n_chunks=t // RET_CHUNK),
        grid=(b, n_heads),
        in_specs=in_specs,
        out_specs=pl.BlockSpec((1, t, LANES), lambda bi, h: (bi, 0, h)),
        out_shape=jax.ShapeDtypeStruct((b, t, n_heads * LANES), MXU_DTYPE),
        scratch_shapes=[pltpu.VMEM((t, LANES), F32)],
        compiler_params=_cparams("parallel", "arbitrary"),
        name="retention",
    )(*args)


def _out_body(od_ref, og_ref, or_ref, wd_ref, wg_ref, wr_ref, x_ref, g_ref, o_ref):
    m = (jnp.dot(od_ref[0], wd_ref[...], preferred_element_type=F32)
         + jnp.dot(og_ref[0], wg_ref[...], preferred_element_type=F32)
         + jnp.dot(or_ref[0], wr_ref[...], preferred_element_type=F32))
    o_ref[0] = x_ref[0] + g_ref[0] * m


def _out_proj(o_diff, o_gqa, o_ret, wd, wg, wr, x, gate, *, per_batch_mod):
    b, t, d = x.shape
    tm = min(t, 1024)
    tn = 512
    mod_map = (lambda bi, i, j: (bi, 0, j)) if per_batch_mod else (lambda bi, i, j: (0, 0, j))
    act = lambda a: pl.BlockSpec((1, tm, a.shape[2]), lambda bi, i, j: (bi, i, 0))
    wsp = lambda w: pl.BlockSpec((w.shape[0], tn), lambda bi, i, j: (0, j))
    return pl.pallas_call(
        _out_body,
        grid=(b, t // tm, d // tn),
        in_specs=[act(o_diff), act(o_gqa), act(o_ret), wsp(wd), wsp(wg), wsp(wr),
                  pl.BlockSpec((1, tm, tn), lambda bi, i, j: (bi, i, j)),
                  pl.BlockSpec((1, 1, tn), mod_map)],
        out_specs=pl.BlockSpec((1, tm, tn), lambda bi, i, j: (bi, i, j)),
        out_shape=jax.ShapeDtypeStruct((b, t, d), F32),
        compiler_params=_cparams("parallel", "parallel", "arbitrary"),
        name="out_proj",
    )(o_diff, o_gqa, o_ret, wd, wg, wr, x, gate)


RANK_ROWS = 24


def _top_ranked(s):
    n = s.shape[1]
    rank = lax.broadcasted_iota(jnp.int32, (RANK_ROWS, n), 0).astype(F32) + 1.0
    rank = jnp.where(rank > PEER_TOPK + 1.0, jnp.inf, rank)
    ranked = jnp.full((RANK_ROWS, n), -jnp.inf, F32)
    count = jnp.zeros((1, n), F32)
    for r in range(PEER_TOPK + 1):
        m = jnp.max(s, axis=0, keepdims=True)
        hit = s >= m
        count = count + jnp.sum(jnp.where(hit, 1.0, 0.0), axis=0, keepdims=True)
        ranked = jnp.maximum(ranked, jnp.where(count >= rank, m, -jnp.inf))
        if r < PEER_TOPK:
            s = jnp.where(hit, -jnp.inf, s)
    return ranked


def _route_body(q_ref, sk_ref, s2_ref, e2_ref, th_ref, e1_ref, *, n_heads):
    for h in range(n_heads):
        s1 = _dot_nt(sk_ref[h, 0], q_ref[:, (2 * h) * LANES:(2 * h + 1) * LANES])
        s2 = _dot_nt(sk_ref[h, 1], q_ref[:, (2 * h + 1) * LANES:(2 * h + 2) * LANES])
        a = _top_ranked(s1)
        b = _top_ranked(s2)
        cand = jnp.concatenate([a[0:1] + b] + [a[i:i + 1] + b[0:8] for i in range(1, 8)]
                               + [a[8:RANK_ROWS] + b[0:1]], axis=0)
        tops = _top_ranked(cand)
        z = jnp.sum(jnp.exp(tops[0:PEER_TOPK] - tops[0:1]), axis=0, keepdims=True)
        tau = 0.5 * (tops[PEER_TOPK - 1:PEER_TOPK] + tops[PEER_TOPK:PEER_TOPK + 1])
        s2_ref[h] = s2
        e2_ref[h] = jnp.exp(s2 - b[0:1])
        th_ref[h] = tau - s1
        e1_ref[h] = jnp.exp(s1 - a[0:1]) / z


def _peer_route(q, sub_keys):
    n, qd = q.shape
    n_heads = sub_keys.shape[0]
    tm = min(n, 256)
    out = jax.ShapeDtypeStruct((n_heads, N_KEYS, n), F32)
    ospec = pl.BlockSpec((n_heads, N_KEYS, tm), lambda i: (0, 0, i))
    return pl.pallas_call(
        functools.partial(_route_body, n_heads=n_heads),
        grid=(n // tm,),
        in_specs=[pl.BlockSpec((tm, qd), lambda i: (i, 0)),
                  pl.BlockSpec(sub_keys.shape, lambda i: (0, 0, 0, 0))],
        out_specs=[ospec] * 4,
        out_shape=[out] * 4,
        compiler_params=_cparams("parallel"),
        name="peer_route",
    )(q, sub_keys)


def _peer_body(h_ref, u_ref, v_ref, s2_ref, e2_ref, th_ref, e1_ref, x_ref, g_ref, o_ref, *, n_heads, te):
    e = pl.program_id(1)
    tm = h_ref.shape[0]
    n_i = te // N_KEYS

    @pl.when(e == 0)
    def _():
        o_ref[...] = jnp.zeros_like(o_ref)

    ht = _dot_nt(u_ref[...], h_ref[...])
    cols = []
    for lc in range(tm // LANES):
        ls = slice(lc * LANES, (lc + 1) * LANES)
        rows = []
        for ii in range(n_i):
            gsum = jnp.zeros((N_KEYS, LANES), F32)
            for hd in range(n_heads):
                th = th_ref[hd, 0, ii:ii + 1, ls]
                e1 = e1_ref[hd, 0, ii:ii + 1, ls]
                gsum = gsum + jnp.where(s2_ref[hd, :, ls] >= th, e2_ref[hd, :, ls], 0.0) * e1
            hh = ht[ii * N_KEYS:(ii + 1) * N_KEYS, ls]
            rows.append(0.5 * hh * (1.0 + lax.erf(hh * (2.0 ** -0.5))) * gsum)
        cols.append(jnp.concatenate(rows, axis=0).T)
    act = jnp.concatenate(cols, axis=0)
    o_ref[...] += _dot(act, v_ref[...])

    @pl.when(e == pl.num_programs(1) - 1)
    def _():
        o_ref[...] = x_ref[...] + g_ref[0] * o_ref[...]


def _peer_experts(h, u, v, s2, e2, th, e1, x, gate, *, tokens_per_batch, per_batch_mod):
    n, d = h.shape
    n_exp = u.shape[0]
    n_heads = s2.shape[0]
    tm = min(tokens_per_batch, 512)
    te = 512
    n_i = te // N_KEYS
    blocks_per_batch = tokens_per_batch // tm
    mod_map = ((lambda t, e: (t // blocks_per_batch, 0, 0)) if per_batch_mod
               else (lambda t, e: (0, 0, 0)))
    full = pl.BlockSpec((n_heads, N_KEYS, tm), lambda t, e: (0, 0, t))
    part = pl.BlockSpec((n_heads, 1, n_i, tm), lambda t, e: (0, e, 0, t))
    by_block = lambda a: a.reshape(n_heads, N_KEYS // n_i, n_i, n)
    return pl.pallas_call(
        functools.partial(_peer_body, n_heads=n_heads, te=te),
        grid=(n // tm, n_exp // te),
        in_specs=[pl.BlockSpec((tm, d), lambda t, e: (t, 0)),
                  pl.BlockSpec((te, d), lambda t, e: (e, 0)),
                  pl.BlockSpec((te, d), lambda t, e: (e, 0)),
                  full, full, part, part,
                  pl.BlockSpec((tm, d), lambda t, e: (t, 0)),
                  pl.BlockSpec((1, 1, d), mod_map)],
        out_specs=pl.BlockSpec((tm, d), lambda t, e: (t, 0)),
        out_shape=jax.ShapeDtypeStruct((n, d), F32),
        compiler_params=_cparams("parallel", "arbitrary"),
        name="peer_experts",
    )(h, u, v, s2, e2, by_block(th), by_block(e1), x, gate)


def _rope_tables(t, head_dim):
    pos = jnp.arange(t)
    row = (pos // GRID_W).astype(F32)
    col = (pos % GRID_W).astype(F32)
    d_axis = head_dim // 2
    inv_freq = ROPE_BASE ** (-jnp.arange(0, d_axis, 2, dtype=F32) / d_axis)
    ang = jnp.concatenate([row[:, None] * inv_freq, col[:, None] * inv_freq], axis=-1)
    cos = jnp.repeat(jnp.cos(ang), 2, axis=-1)
    sin = jnp.repeat(jnp.sin(ang), 2, axis=-1) * jnp.tile(jnp.array([-1.0, 1.0], F32), head_dim // 2)
    reps = LANES // head_dim
    return jnp.tile(cos, (1, reps)), jnp.tile(sin, (1, reps))


def _peer(x, g_norm, sh, sc, gate, w_q, sub_keys, u, v, *, per_batch_mod):
    b, t, d = x.shape
    q, h = _normmod_matmul(x, g_norm, sh, sc, w_q, per_batch_mod=per_batch_mod, emit_h=True, tn=512)
    s2, e2, th, e1 = _peer_route(q.reshape(b * t, -1), sub_keys)
    out = _peer_experts(h.reshape(b * t, d), u, v, s2, e2, th, e1, x.reshape(b * t, d), gate,
                        tokens_per_batch=t, per_batch_mod=per_batch_mod)
    return out.reshape(b, t, d)


def kernel(x, c, ctx, c_ctx, norm1_g, norm2_g, w_mod, b_mod, w_in, w_out,
           diff_q_norm_g, diff_k_norm_g, diff_lambda, diff_out_norm_g,
           gqa_q_norm_g, gqa_k_norm_g, ret_decay_logit, ret_out_norm_g,
           peer_w_q, peer_sub_keys, peer_u, peer_v):
    depth = w_in.shape[0]
    b, t, d = x.shape
    diff_w, gqa_w, ret_w = d // 4, d // 2, d - d // 4 - d // 2
    n_dh, n_gh, n_rh = diff_w // HEAD_W, gqa_w // HEAD_W, ret_w // HEAD_W
    n_kv = n_gh // GQA_GROUP
    assert ret_w % (2 * LANES) == 0 and t % RET_CHUNK == 0 and ctx.shape[1] % RET_CHUNK == 0
    blk = lambda cols: cols // LANES
    q_cols = diff_w + gqa_w + ret_w // 2 + ret_w
    dq0, gq0, rq0, rg0 = 0, blk(diff_w), blk(diff_w + gqa_w), blk(diff_w + gqa_w + ret_w // 2)
    dk0 = blk(q_cols)
    dv0 = dk0 + n_dh
    gk0 = dv0 + n_dh
    gv0 = gk0 + n_kv
    rk0 = gv0 + n_kv
    rv0 = rk0 + n_rh // 2

    rope_d = _rope_tables(t, HALF_W)
    rope_g = _rope_tables(t, HEAD_W)
    c16 = jnp.concatenate([c, c_ctx[None], jnp.zeros((16 - b - 1, d), F32)], axis=0)
    tile2 = lambda g: jnp.tile(g, 2).reshape(1, LANES)
    row = lambda g: g.reshape(1, LANES)

    for l in range(depth):
        update_ctx = l < depth - 1
        lam_init = 0.8 - 0.6 * math.exp(-0.3 * l)
        mod = _modulation(c16, w_mod[l], b_mod[l])
        lat_mod = [mod[:b, k * d:(k + 1) * d].reshape(b, 1, d) for k in range(6)]
        ctx_mod = [mod[b:b + 1, k * d:(k + 1) * d].reshape(1, 1, d) for k in range(6)]
        w_in_l = w_in[l].astype(MXU_DTYPE)
        w_out_l = w_out[l].astype(MXU_DTYPE)
        wd, wg, wr = w_out_l[:diff_w], w_out_l[diff_w:diff_w + gqa_w], w_out_l[diff_w + gqa_w:]
        u_l = peer_u[l].astype(MXU_DTYPE)
        v_l = peer_v[l].astype(MXU_DTYPE)
        wq_l = peer_w_q[l].astype(MXU_DTYPE)
        dl = jnp.broadcast_to(ret_decay_logit[l][:, :, None, None], (2, n_rh, 8, LANES))
        gdq, gdk, gdo = tile2(diff_q_norm_g[l]), tile2(diff_k_norm_g[l]), row(diff_out_norm_g[l])
        ggq, ggk, gro = row(gqa_q_norm_g[l]), row(gqa_k_norm_g[l]), row(ret_out_norm_g[l])

        (p_lat,) = _normmod_matmul(x, norm1_g[l], lat_mod[0], lat_mod[1], w_in_l,
                                   per_batch_mod=True, emit_h=False, tn=768)
        (p_ctx,) = _normmod_matmul(ctx, norm1_g[l], ctx_mod[0], ctx_mod[1], w_in_l,
                                   per_batch_mod=False, emit_h=False, tn=768)

        o_diff = _diff_attention(diff_lambda[l], p_lat, dq0, p_lat, p_ctx, dk0, dv0, rope_d,
                                 gdq, gdk, gdo, lam_init=lam_init, n_heads=n_dh)
        o_gqa = _gqa_attention(p_lat, gq0, p_lat, p_ctx, gk0, gv0, rope_g, ggq, ggk, n_kv=n_kv)
        o_ret = _retention(dl, p_lat, p_ctx, rq0, rg0, rk0, rv0, gro, n_heads=n_rh)
        x = _out_proj(o_diff, o_gqa, o_ret, wd, wg, wr, x, lat_mod[2], per_batch_mod=True)
        x = _peer(x, norm2_g[l], lat_mod[3], lat_mod[4], lat_mod[5], wq_l, peer_sub_keys[l], u_l, v_l,
                  per_batch_mod=True)

        if update_ctx:
            c_diff = _diff_attention(diff_lambda[l], p_ctx, dq0, None, p_ctx, dk0, dv0, None,
                                     gdq, gdk, gdo, lam_init=lam_init, n_heads=n_dh)
            c_gqa = _gqa_attention(p_ctx, gq0, None, p_ctx, gk0, gv0, None, ggq, ggk, n_kv=n_kv)
            c_ret = _retention(dl, p_ctx, None, rq0, rg0, rk0, rv0, gro, n_heads=n_rh)
            ctx = _out_proj(c_diff, c_gqa, c_ret, wd, wg, wr, ctx, ctx_mod[2], per_batch_mod=False)
            ctx = _peer(ctx, norm2_g[l], ctx_mod[3], ctx_mod[4], ctx_mod[5], wq_l, peer_sub_keys[l],
                        u_l, v_l, per_batch_mod=False)
    return x
```
